```python
import jax, jax.numpy as jnp
from jax import lax
import numpy as np

D_MODEL = 1024
BATCH = 8
SEQ = 2048
DEPTH = 1

MLA_HEADS = 8
MLA_NOPE = 64
MLA_ROPE = 32
MLA_V = 64
MLA_QK = MLA_NOPE + MLA_ROPE
MLA_Q_RANK = 256
MLA_KV_RANK = 128
RET_HEADS = 4
RET_DK = 64
RET_DV = 128
CHUNK = 128
Q_BLOCK = 128
D_FF = 4 * D_MODEL
ROPE_BASE = 10000.0
EPS = 1e-5
MLA_WIDTH = MLA_HEADS * MLA_V
RET_WIDTH = RET_HEADS * RET_DV
MIX_WIDTH = MLA_WIDTH + RET_WIDTH
ALPHA = (2.0 * DEPTH) ** 0.25
BETA = (8.0 * DEPTH) ** -0.25
IN_SPLITS = (MLA_Q_RANK, MLA_KV_RANK, MLA_ROPE,
             RET_HEADS * RET_DK, RET_HEADS * RET_DK, RET_WIDTH, RET_WIDTH)
IN_WIDTH = sum(IN_SPLITS)
IN_OFFSETS = tuple(int(v) for v in np.cumsum(IN_SPLITS)[:-1])

kernel_name = "hymba_mla_retention_deepnorm_adaln"


def layer_norm(x, g, b):
    xf = x.astype(jnp.float32)
    mu = jnp.mean(xf, -1, keepdims=True)
    var = jnp.mean(jnp.square(xf - mu), -1, keepdims=True)
    return ((xf - mu) * lax.rsqrt(var + EPS) * g.astype(jnp.float32) + b.astype(jnp.float32)).astype(x.dtype)


def rms_norm(x, g):
    xf = x.astype(jnp.float32)
    ms = jnp.mean(jnp.square(xf), -1, keepdims=True)
    return (xf * lax.rsqrt(ms + EPS) * g.astype(jnp.float32)).astype(x.dtype)


def rotary(x, pos):
    half = x.shape[-1] // 2
    inv = ROPE_BASE ** (-jnp.arange(half, dtype=jnp.float32) / half)
    ang = pos.astype(jnp.float32)[..., None] * inv
    cos = jnp.cos(ang)[:, :, None, :]
    sin = jnp.sin(ang)[:, :, None, :]
    x1 = x[..., :half].astype(jnp.float32)
    x2 = x[..., half:].astype(jnp.float32)
    return jnp.concatenate([x1 * cos - x2 * sin, x2 * cos + x1 * sin], -1).astype(x.dtype)


def causal_block_attention(q, k, v):
    b, s, h, dq = q.shape
    dv = v.shape[-1]
    nb = s // Q_BLOCK
    scale = dq ** -0.5
    qb = q.reshape(b, nb, Q_BLOCK, h, dq).transpose(1, 0, 3, 2, 4)
    kt = k.transpose(0, 2, 1, 3)
    vt = v.transpose(0, 2, 1, 3)
    k_idx = jnp.arange(s)

    def one_block(args):
        q_blk, i = args
        sc = jnp.einsum('bhqd,bhkd->bhqk', q_blk, kt,
                        preferred_element_type=jnp.float32) * scale
        q_idx = i * Q_BLOCK + jnp.arange(Q_BLOCK)
        mask = k_idx[None, :] <= q_idx[:, None]
        p = jax.nn.softmax(jnp.where(mask, sc, -jnp.inf), axis=-1)
        return jnp.einsum('bhqk,bhkd->bhqd', p.astype(vt.dtype), vt)

    out = lax.map(one_block, (qb, jnp.arange(nb)))
    return out.transpose(1, 0, 3, 2, 4).reshape(b, s, h * dv)


def mla_mixer(q_c, kv_c, k_r, pos, g_q, w_uq, g_kv, w_ukv):
    b, s, _ = q_c.shape
    q = (rms_norm(q_c, g_q) @ w_uq).reshape(b, s, MLA_HEADS, MLA_QK)
    q = jnp.concatenate([q[..., :MLA_NOPE], rotary(q[..., MLA_NOPE:], pos)], -1)
    kv = (rms_norm(kv_c, g_kv) @ w_ukv).reshape(b, s, MLA_HEADS, MLA_NOPE + MLA_V)
    k_nope, v = kv[..., :MLA_NOPE], kv[..., MLA_NOPE:]
    k_rope = rotary(k_r[:, :, None, :], pos)
    k = jnp.concatenate([k_nope, jnp.broadcast_to(k_rope, (b, s, MLA_HEADS, MLA_ROPE))], -1)
    return causal_block_attention(q, k, v)


def retention_mixer(q, k, v, g, pos, gn_g, gn_b):
    b, s, _ = q.shape
    n = s // CHUNK
    q = rotary(q.reshape(b, s, RET_HEADS, RET_DK), pos)
    k = rotary(k.reshape(b, s, RET_HEADS, RET_DK), pos) * (RET_DK ** -0.5)
    v = v.reshape(b, s, RET_HEADS, RET_DV)
    qc = q.reshape(b, n, CHUNK, RET_HEADS, RET_DK).transpose(0, 3, 1, 2, 4)
    kc = k.reshape(b, n, CHUNK, RET_HEADS, RET_DK).transpose(0, 3, 1, 2, 4)
    vc = v.reshape(b, n, CHUNK, RET_HEADS, RET_DV).transpose(0, 3, 1, 2, 4)

    log_g = jnp.log(1.0 - 2.0 ** (-5.0 - jnp.arange(RET_HEADS, dtype=jnp.float32)))
    idx = jnp.arange(CHUNK, dtype=jnp.float32)
    diff = idx[:, None] - idx[None, :]
    decay = jnp.where(diff >= 0, jnp.exp(log_g[:, None, None] * jnp.maximum(diff, 0.0)), 0.0)
    zeta = jnp.exp(log_g[:, None] * (CHUNK - 1 - idx))
    xi = jnp.exp(log_g[:, None] * (idx + 1.0))

    scores = jnp.einsum('bhnid,bhnjd->bhnij', qc, kc) * decay[None, :, None]
    inner = jnp.einsum('bhnij,bhnjv->bhniv', scores, vc)
    chunk_kv = jnp.einsum('bhnjd,hj,bhnjv->bhndv', kc, zeta, vc)
    g_chunk = jnp.exp(log_g * CHUNK).astype(chunk_kv.dtype)[None, :, None, None]

    def step(state, kv_n):
        return state * g_chunk + kv_n, state

    init = jnp.zeros((b, RET_HEADS, RET_DK, RET_DV), chunk_kv.dtype)
    _, prev = lax.scan(step, init, chunk_kv.transpose(2, 0, 1, 3, 4))
    prev = prev.transpose(1, 2, 0, 3, 4)
    cross = jnp.einsum('bhnid,bhndv->bhniv', qc, prev) * xi[None, :, None, :, None]

    o = (inner + cross).transpose(0, 2, 3, 1, 4).reshape(b, s, RET_HEADS, RET_DV)
    of = o.astype(jnp.float32)
    mu = jnp.mean(of, -1, keepdims=True)
    var = jnp.mean(jnp.square(of - mu), -1, keepdims=True)
    on = ((of - mu) * lax.rsqrt(var + EPS)).reshape(b, s, RET_WIDTH)
    on = on * gn_g.astype(jnp.float32) + gn_b.astype(jnp.float32)
    return (on * jax.nn.silu(g.astype(jnp.float32))).astype(g.dtype)


def setup_inputs(seed: int = 0) -> dict:
    key = jax.random.key(seed)
    ks = jax.random.split(key, 24)
    f32 = jnp.float32
    nrm = lambda k, shape, fan: jax.random.normal(k, shape, f32) * (fan ** -0.5)
    gain = lambda k, shape: 1.0 + 0.05 * jax.random.normal(k, shape, f32)
    bias = lambda k, shape: 0.02 * jax.random.normal(k, shape, f32)

    x = jax.random.normal(ks[0], (BATCH, SEQ, D_MODEL), f32)
    c = jax.random.normal(ks[1], (BATCH, D_MODEL), f32)
    offset = jax.random.randint(ks[2], (BATCH, 1), 0, 1024, dtype=jnp.int32)
    positions = (offset + jnp.arange(SEQ, dtype=jnp.int32)[None, :]).astype(jnp.int32)

    in_col_scale = jnp.concatenate([
        jnp.ones((IN_WIDTH - 2 * RET_WIDTH,), f32),
        jnp.full((RET_WIDTH,), BETA, f32),
        jnp.ones((RET_WIDTH,), f32)])
    ukv_col_scale = jnp.tile(jnp.concatenate([jnp.ones((MLA_NOPE,), f32),
                                              jnp.full((MLA_V,), BETA, f32)]), MLA_HEADS)
    return {
        "x": x,
        "c": c,
        "positions": positions,
        "ln_in_g": gain(ks[3], (D_MODEL,)),
        "ln_in_b": bias(ks[4], (D_MODEL,)),
        "w_ada": nrm(ks[5], (DEPTH, D_MODEL, 6 * D_MODEL), D_MODEL) * 0.5,
        "b_ada": bias(ks[6], (DEPTH, 6 * D_MODEL)),
        "w_in": nrm(ks[7], (DEPTH, D_MODEL, IN_WIDTH), D_MODEL) * in_col_scale,
        "mla_q_norm": gain(ks[8], (DEPTH, MLA_Q_RANK)),
        "w_uq": nrm(ks[9], (DEPTH, MLA_Q_RANK, MLA_HEADS * MLA_QK), MLA_Q_RANK),
        "mla_kv_norm": gain(ks[10], (DEPTH, MLA_KV_RANK)),
        "w_ukv": nrm(ks[11], (DEPTH, MLA_KV_RANK, MLA_HEADS * (MLA_NOPE + MLA_V)), MLA_KV_RANK) * ukv_col_scale,
        "ret_gn_g": gain(ks[12], (DEPTH, RET_WIDTH)),
        "ret_gn_b": bias(ks[13], (DEPTH, RET_WIDTH)),
        "w_out": nrm(ks[14], (DEPTH, MIX_WIDTH, D_MODEL), MIX_WIDTH) * BETA,
        "ln1_g": gain(ks[15], (DEPTH, D_MODEL)),
        "ln1_b": bias(ks[16], (DEPTH, D_MODEL)),
        "w_ff1": nrm(ks[17], (DEPTH, D_MODEL, D_FF), D_MODEL) * BETA,
        "w_ff2": nrm(ks[18], (DEPTH, D_FF, D_MODEL), D_FF) * BETA,
        "ln2_g": gain(ks[19], (DEPTH, D_MODEL)),
        "ln2_b": bias(ks[20], (DEPTH, D_MODEL)),
    }


def reference(x, c, positions, ln_in_g, ln_in_b, w_ada, b_ada, w_in, mla_q_norm, w_uq,
              mla_kv_norm, w_ukv, ret_gn_g, ret_gn_b, w_out, ln1_g, ln1_b,
              w_ff1, w_ff2, ln2_g, ln2_b):
    x = layer_norm(x, ln_in_g, ln_in_b)
    c_act = jax.nn.silu(c)
    for l in range(DEPTH):
        mod = c_act @ w_ada[l] + b_ada[l]
        sh1, sc1, gt1, sh2, sc2, gt2 = [m[:, None, :] for m in jnp.split(mod, 6, axis=-1)]

        h = x * (1.0 + sc1) + sh1
        proj = h @ w_in[l]
        q_c, kv_c, k_r, r_q, r_k, r_v, r_g = jnp.split(proj, IN_OFFSETS, axis=-1)
        a = mla_mixer(q_c, kv_c, k_r, positions, mla_q_norm[l], w_uq[l], mla_kv_norm[l], w_ukv[l])
        r = retention_mixer(r_q, r_k, r_v, r_g, positions, ret_gn_g[l], ret_gn_b[l])
        y = jnp.concatenate([a, r], axis=-1) @ w_out[l]
        x = layer_norm(ALPHA * x + gt1 * y, ln1_g[l], ln1_b[l])

        h = x * (1.0 + sc2) + sh2
        f = jnp.square(jax.nn.relu(h @ w_ff1[l])) @ w_ff2[l]
        x = layer_norm(ALPHA * x + gt2 * f, ln2_g[l], ln2_b[l])
    return x
```

```python
import functools
import math

import numpy as np
import jax
import jax.numpy as jnp
from jax import lax
from jax.experimental import pallas as pl
from jax.experimental.pallas import tpu as pltpu

D_MODEL = 1024
BATCH = 8
SEQ = 2048
MLA_HEADS = 8
MLA_NOPE = 64
MLA_ROPE = 32
MLA_V = 64
MLA_QK = MLA_NOPE + MLA_ROPE
MLA_Q_RANK = 256
MLA_KV_RANK = 128
RET_HEADS = 4
RET_DK = 64
RET_DV = 128
CHUNK = 128
D_FF = 4 * D_MODEL
ROPE_BASE = 10000.0
EPS = 1e-5
MLA_WIDTH = MLA_HEADS * MLA_V
RET_WIDTH = RET_HEADS * RET_DV
DEPTH = 1
ALPHA = (2.0 * DEPTH) ** 0.25

LANES = 128
HEAD_SLAB = LANES
MLA_PAD = MLA_HEADS * HEAD_SLAB
IN_PAD = 2048
TOK_TILE = 512
ATT_TILE = 512
FF_CHUNK = 1024
VMEM_LIMIT = 56 * 1024 * 1024
NEG_BIG = -1e30
F32 = jnp.float32
BF16 = jnp.bfloat16

_QC, _KVC, _KR, _RQ, _RK, _RV, _RG = 0, 256, 384, 512, 768, 1024, 1536


def _const_spec(shape):
    nd = len(shape)
    return pl.BlockSpec(shape, lambda *_: (0,) * nd, pipeline_mode=pl.Buffered(1))


def _layer_norm(x, g, b):
    mu = jnp.mean(x, axis=-1, keepdims=True)
    xc = x - mu
    var = jnp.mean(xc * xc, axis=-1, keepdims=True)
    return xc * lax.rsqrt(var + EPS) * g + b


def _rms_norm(x, g):
    ms = jnp.mean(x * x, axis=-1, keepdims=True)
    return x * lax.rsqrt(ms + EPS) * g


def _ada_kernel(c_ref, w_ref, b_ref, o_ref):
    c = c_ref[...]
    c_act = c * (1.0 / (1.0 + jnp.exp(-c)))
    o_ref[...] = jnp.dot(c_act.astype(BF16), w_ref[0].astype(BF16),
                         preferred_element_type=F32) + b_ref[...]


def _ada_mod(c, w_ada, b_ada):
    n = 6 * D_MODEL
    blk = D_MODEL
    return pl.pallas_call(
        _ada_kernel,
        grid=(n // blk,),
        in_specs=[pl.BlockSpec((BATCH, D_MODEL), lambda j: (0, 0)),
                  pl.BlockSpec((1, D_MODEL, blk), lambda j: (0, 0, j)),
                  pl.BlockSpec((1, blk), lambda j: (0, j))],
        out_specs=pl.BlockSpec((BATCH, blk), lambda j: (0, j)),
        out_shape=jax.ShapeDtypeStruct((BATCH, n), F32),
        compiler_params=pltpu.CompilerParams(dimension_semantics=("arbitrary",),
                                             vmem_limit_bytes=VMEM_LIMIT),
        name="ada_mod",
    )(c, w_ada, b_ada)


def _rotate(x, cos_t, sin_up, sin_dn, shift):
    return (x * cos_t + pltpu.roll(x, shift, 1) * sin_up
            + pltpu.roll(x, LANES - shift, 1) * sin_dn)


def _in_kernel(x_ref, pos_ref, mod_ref, lng_ref, lnb_ref, gq_ref, gkv_ref, inv_ref,
               win_ref, wuq_ref, wk_ref, wv_ref,
               q_ref, k_ref, v_ref, rq_ref, rk_ref, rv_ref, rg_ref, *, q_scale):
    x = x_ref[0]
    xn = _layer_norm(x, lng_ref[...], lnb_ref[...])
    mod = mod_ref[0]
    sh1 = mod[:, 0:D_MODEL]
    sc1 = mod[:, D_MODEL:2 * D_MODEL]
    h = (xn * (1.0 + sc1) + sh1).astype(BF16)
    proj = jnp.dot(h, win_ref[...], preferred_element_type=F32)

    qn = _rms_norm(proj[:, _QC:_QC + MLA_Q_RANK], gq_ref[...]).astype(BF16)
    kvn = _rms_norm(proj[:, _KVC:_KVC + MLA_KV_RANK], gkv_ref[...]).astype(BF16)
    q = jnp.dot(qn, wuq_ref[...], preferred_element_type=F32)
    kn = jnp.dot(kvn, wk_ref[...], preferred_element_type=F32)
    v = jnp.dot(kvn, wv_ref[...], preferred_element_type=F32)
    v_ref[0] = v.astype(BF16)

    pos = pos_ref[0].astype(F32)
    lane = lax.broadcasted_iota(jnp.int32, (1, LANES), 1)

    ang = pos * inv_ref[0:1, :]
    cos_m = jnp.cos(ang)
    sin_m = jnp.sin(ang)
    half = MLA_ROPE // 2
    up_m = jnp.where((lane >= MLA_NOPE + half) & (lane < MLA_QK), sin_m, 0.0)
    dn_m = jnp.where((lane >= MLA_NOPE) & (lane < MLA_NOPE + half), -sin_m, 0.0)
    kr = _rotate(proj[:, _KR:_KR + LANES], cos_m, up_m, dn_m, half)
    cos_q, up_q, dn_q = cos_m * q_scale, up_m * q_scale, dn_m * q_scale
    for hd in range(MLA_HEADS):
        sl = slice(hd * HEAD_SLAB, (hd + 1) * HEAD_SLAB)
        q_ref[0, :, sl] = _rotate(q[:, sl], cos_q, up_q, dn_q, half).astype(BF16)
        k_ref[0, :, sl] = (kn[:, sl] + kr).astype(BF16)

    ang = pos * inv_ref[1:2, :]
    cos_r = jnp.cos(ang)
    sin_r = jnp.sin(ang)
    halfr = RET_DK // 2
    in_hi = (lane & (RET_DK - 1)) >= halfr
    up_r = jnp.where(in_hi, sin_r, 0.0)
    dn_r = jnp.where(in_hi, 0.0, -sin_r)
    k_scale = RET_DK ** -0.5
    cos_k, up_k, dn_k = cos_r * k_scale, up_r * k_scale, dn_r * k_scale
    for t in range(RET_HEADS * RET_DK // LANES):
        sl = slice(t * LANES, (t + 1) * LANES)
        rq_ref[0, :, sl] = _rotate(proj[:, _RQ + t * LANES:_RQ + (t + 1) * LANES],
                                   cos_r, up_r, dn_r, halfr).astype(BF16)
        rk_ref[0, :, sl] = _rotate(proj[:, _RK + t * LANES:_RK + (t + 1) * LANES],
                                   cos_k, up_k, dn_k, halfr).astype(BF16)
    rv_ref[0] = proj[:, _RV:_RV + RET_WIDTH].astype(BF16)
    rg_ref[0] = proj[:, _RG:_RG + RET_WIDTH]


def _in_stage(x, pos3, mod3, lng, lnb, gq, gkv, inv_tab, w_in, w_uq, w_k, w_v):
    t = TOK_TILE
    grid = (BATCH, SEQ // t)
    tok = lambda w: pl.BlockSpec((1, t, w), lambda b, i: (b, i, 0))
    out_shape = (
        jax.ShapeDtypeStruct((BATCH, SEQ, MLA_PAD), BF16),
        jax.ShapeDtypeStruct((BATCH, SEQ, MLA_PAD), BF16),
        jax.ShapeDtypeStruct((BATCH, SEQ, MLA_PAD), BF16),
        jax.ShapeDtypeStruct((BATCH, SEQ, RET_HEADS * RET_DK), BF16),
        jax.ShapeDtypeStruct((BATCH, SEQ, RET_HEADS * RET_DK), BF16),
        jax.ShapeDtypeStruct((BATCH, SEQ, RET_WIDTH), BF16),
        jax.ShapeDtypeStruct((BATCH, SEQ, RET_WIDTH), F32),
    )
    q_scale = MLA_QK ** -0.5 * math.log2(math.e)
    return pl.pallas_call(
        functools.partial(_in_kernel, q_scale=q_scale),
        grid=grid,
        in_specs=[tok(D_MODEL), tok(1),
                  pl.BlockSpec((1, 1, 6 * D_MODEL), lambda b, i: (b, 0, 0)),
                  _const_spec((1, D_MODEL)), _const_spec((1, D_MODEL)),
                  _const_spec((1, MLA_Q_RANK)), _const_spec((1, MLA_KV_RANK)),
                  _const_spec((8, LANES)),
                  _const_spec((D_MODEL, IN_PAD)), _const_spec((MLA_Q_RANK, MLA_PAD)),
                  _const_spec((MLA_KV_RANK, MLA_PAD)), _const_spec((MLA_KV_RANK, MLA_PAD))],
        out_specs=(tok(MLA_PAD), tok(MLA_PAD), tok(MLA_PAD),
                   tok(RET_HEADS * RET_DK), tok(RET_HEADS * RET_DK),
                   tok(RET_WIDTH), tok(RET_WIDTH)),
        out_shape=out_shape,
        compiler_params=pltpu.CompilerParams(dimension_semantics=("arbitrary", "arbitrary"),
                                             vmem_limit_bytes=VMEM_LIMIT),
        name="in_stage",
    )(x, pos3, mod3, lng, lnb, gq, gkv, inv_tab, w_in, w_uq, w_k, w_v)


def _attn_kernel(q_ref, k_ref, v_ref, o_ref):
    t = ATT_TILE
    i = pl.program_id(1)
    row = lax.broadcasted_iota(jnp.int32, (t, t), 0)
    col = lax.broadcasted_iota(jnp.int32, (t, t), 1)
    causal = col <= row
    nt = (((1,), (1,)), ((), ()))

    def head(hd):
        sl = slice(hd * HEAD_SLAB, (hd + 1) * HEAD_SLAB)
        qh = q_ref[0, :, sl]

        def block(j, carry, masked):
            m, l, acc = carry
            rows = pl.ds(pl.multiple_of(j * t, t), t)
            s = lax.dot_general(qh, k_ref[0, rows, sl], nt, preferred_element_type=F32)
            if masked:
                s = jnp.where(causal, s, NEG_BIG)
            m_new = jnp.maximum(m, jnp.max(s, axis=1, keepdims=True))
            p = jnp.exp2(s - m_new)
            a = jnp.exp2(m - m_new)
            l = a * l + jnp.sum(p, axis=1, keepdims=True)
            acc = a * acc + jnp.dot(p.astype(BF16), v_ref[0, rows, sl],
                                    preferred_element_type=F32)
            return m_new, l, acc

        init = (jnp.full((t, 1), NEG_BIG, F32), jnp.zeros((t, 1), F32),
                jnp.zeros((t, HEAD_SLAB), F32))
        carry = lax.fori_loop(0, i, lambda j, c: block(j, c, False), init)
        _, l, acc = block(i, carry, True)
        return acc / l

    for p in range(MLA_HEADS // 2):
        o_ref[0, :, p * LANES:(p + 1) * LANES] = (head(2 * p) + head(2 * p + 1)).astype(BF16)


def _attention(q, k, v):
    t = ATT_TILE
    return pl.pallas_call(
        _attn_kernel,
        grid=(BATCH, SEQ // t),
        in_specs=[pl.BlockSpec((1, t, MLA_PAD), lambda b, i: (b, i, 0)),
                  pl.BlockSpec((1, SEQ, MLA_PAD), lambda b, i: (b, 0, 0)),
                  pl.BlockSpec((1, SEQ, MLA_PAD), lambda b, i: (b, 0, 0))],
        out_specs=pl.BlockSpec((1, t, MLA_WIDTH), lambda b, i: (b, i, 0)),
        out_shape=jax.ShapeDtypeStruct((BATCH, SEQ, MLA_WIDTH), BF16),
        compiler_params=pltpu.CompilerParams(dimension_semantics=("arbitrary", "arbitrary"),
                                             vmem_limit_bytes=VMEM_LIMIT),
        name="mla_attention",
    )(q, k, v)


def _ret_kernel(q_ref, k_ref, v_ref, g_ref, tab_ref, gng_ref, gnb_ref, o_ref):
    lane = lax.broadcasted_iota(jnp.int32, (1, LANES), 1)
    nt = (((1,), (1,)), ((), ()))
    n_chunks = SEQ // CHUNK

    def chunk(c, states):
        rows = pl.ds(pl.multiple_of(c * CHUNK, CHUNK), CHUNK)
        qc2 = q_ref[0, rows, :]
        kc2 = k_ref[0, rows, :]
        new_states = []
        for hh in range(2):
            decay, zeta, xi, gch = (tab_ref[hh, n] for n in range(4))
            own = (lane >= hh * RET_DK) & (lane < (hh + 1) * RET_DK)
            qc = jnp.where(own, qc2, jnp.zeros_like(qc2))
            vc = v_ref[0, rows, hh * RET_DV:(hh + 1) * RET_DV]
            state = states[hh]
            scores = lax.dot_general(qc, kc2, nt, preferred_element_type=F32) * decay
            inner = jnp.dot(scores.astype(BF16), vc, preferred_element_type=F32)
            cross = jnp.dot(qc, state.astype(BF16), preferred_element_type=F32) * xi
            kz = (kc2.astype(F32) * zeta).T.astype(BF16)
            new_states.append(state * gch + jnp.dot(kz, vc, preferred_element_type=F32))
            o = inner + cross
            mu = jnp.mean(o, axis=-1, keepdims=True)
            oc = o - mu
            var = jnp.mean(oc * oc, axis=-1, keepdims=True)
            sl = slice(hh * RET_DV, (hh + 1) * RET_DV)
            on = oc * lax.rsqrt(var + EPS) * gng_ref[:, sl] + gnb_ref[:, sl]
            g = g_ref[0, rows, sl]
            o_ref[0, rows, sl] = (on * (g * (1.0 / (1.0 + jnp.exp(-g))))).astype(BF16)
        return tuple(new_states)

    zero = jnp.zeros((LANES, RET_DV), F32)
    lax.fori_loop(0, n_chunks, chunk, (zero, zero))


def _retention(rq, rk, rv, rg, tab, gng, gnb):
    pair_w = 2 * RET_DV
    return pl.pallas_call(
        _ret_kernel,
        grid=(BATCH, RET_HEADS // 2),
        in_specs=[pl.BlockSpec((1, SEQ, LANES), lambda b, p: (b, 0, p)),
                  pl.BlockSpec((1, SEQ, LANES), lambda b, p: (b, 0, p)),
                  pl.BlockSpec((1, SEQ, pair_w), lambda b, p: (b, 0, p)),
                  pl.BlockSpec((1, SEQ, pair_w), lambda b, p: (b, 0, p)),
                  pl.BlockSpec((2, 4, CHUNK, LANES), lambda b, p: (p, 0, 0, 0)),
                  pl.BlockSpec((1, pair_w), lambda b, p: (0, p)),
                  pl.BlockSpec((1, pair_w), lambda b, p: (0, p))],
        out_specs=pl.BlockSpec((1, SEQ, pair_w), lambda b, p: (b, 0, p)),
        out_shape=jax.ShapeDtypeStruct((BATCH, SEQ, RET_WIDTH), BF16),
        compiler_params=pltpu.CompilerParams(dimension_semantics=("arbitrary", "arbitrary"),
                                             vmem_limit_bytes=VMEM_LIMIT),
        name="retention",
    )(rq, rk, rv, rg, tab, gng, gnb)


def _out_kernel(x_ref, a_ref, r_ref, mod_ref, lng_ref, lnb_ref, g1_ref, b1_ref, g2_ref, b2_ref,
                wo_ref, w1_ref, w2_ref, o_ref):
    xn = _layer_norm(x_ref[0], lng_ref[...], lnb_ref[...])
    mod = mod_ref[0]
    gt1 = mod[:, 2 * D_MODEL:3 * D_MODEL]
    sh2 = mod[:, 3 * D_MODEL:4 * D_MODEL]
    sc2 = mod[:, 4 * D_MODEL:5 * D_MODEL]
    gt2 = mod[:, 5 * D_MODEL:6 * D_MODEL]
    y = (jnp.dot(a_ref[0], wo_ref[0:MLA_WIDTH, :], preferred_element_type=F32)
         + jnp.dot(r_ref[0], wo_ref[MLA_WIDTH:, :], preferred_element_type=F32))
    x1 = _layer_norm(ALPHA * xn + gt1 * y, g1_ref[...], b1_ref[...])
    h = (x1 * (1.0 + sc2) + sh2).astype(BF16)
    f = None
    for c in range(D_FF // FF_CHUNK):
        cs = slice(c * FF_CHUNK, (c + 1) * FF_CHUNK)
        u = jnp.maximum(jnp.dot(h, w1_ref[:, cs], preferred_element_type=F32), 0.0)
        part = jnp.dot((u * u).astype(BF16), w2_ref[cs, :], preferred_element_type=F32)
        f = part if f is None else f + part
    o_ref[0] = _layer_norm(ALPHA * x1 + gt2 * f, g2_ref[...], b2_ref[...])


def _out_stage(x, a, r, mod3, lng, lnb, g1, b1, g2, b2, w_out, w_ff1, w_ff2):
    t = TOK_TILE
    tok = lambda w: pl.BlockSpec((1, t, w), lambda b, i: (b, i, 0))
    vec = _const_spec((1, D_MODEL))
    return pl.pallas_call(
        _out_kernel,
        grid=(BATCH, SEQ // t),
        in_specs=[tok(D_MODEL), tok(MLA_WIDTH), tok(RET_WIDTH),
                  pl.BlockSpec((1, 1, 6 * D_MODEL), lambda b, i: (b, 0, 0)),
                  vec, vec, vec, vec, vec, vec,
                  _const_spec((D_MODEL, D_MODEL)), _const_spec((D_MODEL, D_FF)),
                  _const_spec((D_FF, D_MODEL))],
        out_specs=tok(D_MODEL),
        out_shape=jax.ShapeDtypeStruct((BATCH, SEQ, D_MODEL), F32),
        compiler_params=pltpu.CompilerParams(dimension_semantics=("arbitrary", "arbitrary"),
                                             vmem_limit_bytes=VMEM_LIMIT),
        name="out_stage",
    )(x, a, r, mod3, lng, lnb, g1, b1, g2, b2, w_out, w_ff1, w_ff2)


def _retention_tables():
    hd = np.arange(RET_HEADS, dtype=np.float64)
    log_g = np.log(1.0 - 2.0 ** (-5.0 - hd))
    idx = np.arange(CHUNK, dtype=np.float64)
    diff = idx[:, None] - idx[None, :]
    decay = np.where(diff >= 0, np.exp(log_g[:, None, None] * np.maximum(diff, 0.0)), 0.0)
    zeta = np.exp(log_g[:, None] * (CHUNK - 1 - idx))
    xi = np.exp(log_g[:, None] * (idx + 1.0))
    gch = np.exp(log_g * CHUNK)
    ones = np.ones((RET_HEADS, CHUNK, LANES))
    tab = np.stack([decay, zeta[:, :, None] * ones, xi[:, :, None] * ones,
                    gch[:, None, None] * ones], axis=1)
    return jnp.asarray(tab, dtype=F32)


def _rope_table():
    def inv(half):
        return ROPE_BASE ** (-jnp.arange(half, dtype=F32) / half)
    z = lambda n: jnp.zeros((n,), F32)
    inv_m = inv(MLA_ROPE // 2)
    row_m = jnp.concatenate([z(MLA_NOPE), inv_m, inv_m, z(HEAD_SLAB - MLA_QK)])
    row_r = jnp.tile(inv(RET_DK // 2), LANES // (RET_DK // 2))
    return jnp.concatenate([row_m[None], row_r[None], jnp.zeros((6, LANES), F32)], axis=0)


def _pad_cols(w, groups, width, padded, offset=None):
    k = w.shape[0]
    w = w.reshape(k, groups, width)
    out = jnp.zeros((k, groups, padded), w.dtype)
    if offset is None:
        return out.at[:, :, :width].set(w).reshape(k, groups * padded)
    for g in range(groups):
        out = out.at[:, g, offset[g]:offset[g] + width].set(w[:, g])
    return out.reshape(k, groups * padded)


def kernel(x, c, positions, ln_in_g, ln_in_b, w_ada, b_ada, w_in, mla_q_norm, w_uq, mla_kv_norm,
           w_ukv, ret_gn_g, ret_gn_b, w_out, ln1_g, ln1_b, w_ff1, w_ff2, ln2_g, ln2_b):
    assert x.shape == (BATCH, SEQ, D_MODEL) and w_in.shape[0] == DEPTH == 1
    wi = w_in[0]
    zc = lambda n: jnp.zeros((D_MODEL, n), wi.dtype)
    w_in_p = jnp.concatenate(
        [wi[:, :384], zc(MLA_NOPE), wi[:, 384:416], zc(HEAD_SLAB - MLA_QK), wi[:, 416:]],
        axis=1).astype(BF16)
    w_uq_p = _pad_cols(w_uq[0], MLA_HEADS, MLA_QK, HEAD_SLAB).astype(BF16)
    ukv = w_ukv[0].reshape(MLA_KV_RANK, MLA_HEADS, MLA_NOPE + MLA_V)
    w_k_p = _pad_cols(ukv[:, :, :MLA_NOPE].reshape(MLA_KV_RANK, -1),
                      MLA_HEADS, MLA_NOPE, HEAD_SLAB).astype(BF16)
    w_v_p = _pad_cols(ukv[:, :, MLA_NOPE:].reshape(MLA_KV_RANK, -1), MLA_HEADS, MLA_V, HEAD_SLAB,
                      offset=[(g % 2) * MLA_V for g in range(MLA_HEADS)]).astype(BF16)
    row = lambda v: v.reshape(1, -1)

    mod = _ada_mod(c, w_ada, b_ada[0:1])
    mod3 = mod.reshape(BATCH, 1, 6 * D_MODEL)
    pos3 = positions.reshape(BATCH, SEQ, 1)
    q, k, v, rq, rk, rv, rg = _in_stage(
        x, pos3, mod3, row(ln_in_g), row(ln_in_b), mla_q_norm, mla_kv_norm, _rope_table(),
        w_in_p, w_uq_p, w_k_p, w_v_p)
    a = _attention(q, k, v)
    r = _retention(rq, rk, rv, rg, _retention_tables(), ret_gn_g, ret_gn_b)
    return _out_stage(x, a, r, mod3, row(ln_in_g), row(ln_in_b), ln1_g, ln1_b, ln2_g, ln2_b,
                      w_out[0].astype(BF16), w_ff1[0].astype(BF16), w_ff2[0].astype(BF16))
```
